```python
import jax, jax.numpy as jnp
from jax import lax
import numpy as np

D_MODEL = 1024
BATCH = 4
SEQ = 8192
DEPTH = 2

GRID_W = 64
CTX_LEN = 256
D_MIX = D_MODEL
N_GROUPS = 4
GROUP_W = D_MIX // N_GROUPS
NA_HEADS = 4
NA_HEAD_DIM = GROUP_W // NA_HEADS
NA_WIN_R_MAX = 8
NA_WIN_C = 16
POOL_WINDOWS = (2, 4, 8, 16)
POOL_CH = GROUP_W // len(POOL_WINDOWS)
FNO_GROUPS = 4
FNO_CH = GROUP_W // FNO_GROUPS
MLA_HEADS = 4
MLA_NOPE = 64
MLA_ROPE = 32
MLA_V = GROUP_W // MLA_HEADS
MLA_Q_RANK = 256
MLA_KV_RANK = 128
ROPE_BASE = 10000.0
Q_BLOCK = 128
PROJ_SIZES = (GROUP_W, GROUP_W, GROUP_W, GROUP_W, GROUP_W, MLA_Q_RANK, MLA_KV_RANK, MLA_ROPE)
D_IN = 5 * GROUP_W + MLA_Q_RANK + MLA_KV_RANK + MLA_ROPE
N_EXPERTS = 16
EC_CAPACITY_FACTOR = 2
D_EXPERT = 1024
EPS = 1e-6

kernel_name = 'hybrid_diffusion_parallel_head_groups'


def _rmsnorm(x, g):
    xf = x.astype(jnp.float32)
    y = xf * lax.rsqrt(jnp.mean(xf * xf, axis=-1, keepdims=True) + EPS)
    return (y * g.astype(jnp.float32)).astype(x.dtype)


def _modulate(x, g, shift, scale):
    return _rmsnorm(x, g) * (1 + scale) + shift


def _adaln(cvec, ada_w, ada_b):
    m = jax.nn.silu(cvec) @ ada_w + ada_b
    return jnp.split(m, 6, axis=-1)


def _rotate(v, ang):
    f = ang.shape[-1]
    cos = jnp.cos(ang)[:, None, :].astype(v.dtype)
    sin = jnp.sin(ang)[:, None, :].astype(v.dtype)
    v1, v2 = v[..., :f], v[..., f:]
    return jnp.concatenate([v1 * cos - v2 * sin, v1 * sin + v2 * cos], axis=-1)


def _rope_2d(v):
    n = v.shape[1]
    t = jnp.arange(n)
    rows = (t // GRID_W).astype(jnp.float32)
    cols = (t % GRID_W).astype(jnp.float32)
    half = v.shape[-1] // 2
    nf = half // 2
    inv = ROPE_BASE ** (-jnp.arange(nf, dtype=jnp.float32) / nf)
    return jnp.concatenate([_rotate(v[..., :half], rows[:, None] * inv),
                            _rotate(v[..., half:], cols[:, None] * inv)], axis=-1)


def _heads(t):
    return t.reshape(t.shape[0], t.shape[1], NA_HEADS, NA_HEAD_DIM)


def _neighbourhood_attn(q, k, v, kc, vc, bias_tab):
    B, N, H, dh = q.shape
    rows = N // GRID_W
    win_r = min(NA_WIN_R_MAX, rows)
    scale = dh ** -0.5
    qg = q.reshape(B, rows, GRID_W, H, dh)
    kg = k.reshape(B, rows, GRID_W, H, dh)
    vg = v.reshape(B, rows, GRID_W, H, dh)
    row_start = jnp.clip(jnp.arange(rows) - win_r // 2, 0, rows - win_r)
    col_start = jnp.clip(jnp.arange(GRID_W) - NA_WIN_C // 2, 0, GRID_W - NA_WIN_C)
    col_idx = col_start[:, None] + jnp.arange(NA_WIN_C)
    dc = col_idx - jnp.arange(GRID_W)[:, None] + (NA_WIN_C - 1)
    nw = win_r * NA_WIN_C

    def row_block(args):
        q_row, qy, rs = args
        k_rows = lax.dynamic_slice_in_dim(kg, rs, win_r, axis=1)
        v_rows = lax.dynamic_slice_in_dim(vg, rs, win_r, axis=1)
        k_win = k_rows[:, :, col_idx]
        v_win = v_rows[:, :, col_idx]
        dr = rs + jnp.arange(win_r) - qy + (NA_WIN_R_MAX - 1)
        bias = bias_tab[:, dr][:, :, dc].transpose(0, 2, 1, 3)
        s_win = (jnp.einsum('bqhd,brqjhd->bhqrj', q_row, k_win).astype(jnp.float32) * scale
                 + bias[None].astype(jnp.float32)).reshape(B, H, GRID_W, nw)
        s_ctx = jnp.einsum('bqhd,blhd->bhql', q_row, kc).astype(jnp.float32) * scale
        p = jax.nn.softmax(jnp.concatenate([s_win, s_ctx], axis=-1), axis=-1)
        p_win = p[..., :nw].reshape(B, H, GRID_W, win_r, NA_WIN_C).astype(v.dtype)
        p_ctx = p[..., nw:].astype(v.dtype)
        return (jnp.einsum('bhqrj,brqjhd->bqhd', p_win, v_win)
                + jnp.einsum('bhql,blhd->bqhd', p_ctx, vc))

    out = lax.map(row_block, (qg.transpose(1, 0, 2, 3, 4), jnp.arange(rows), row_start))
    return out.transpose(1, 0, 2, 3, 4).reshape(B, N, H * dh)


def _block_attn(q, k, v, kc, vc):
    B, N, H, d = q.shape
    dv = v.shape[-1]
    scale = d ** -0.5
    nb = N // Q_BLOCK
    qb = q.reshape(B, nb, Q_BLOCK, H, d).transpose(1, 0, 2, 3, 4)

    def body(qblk):
        s_lat = jnp.einsum('bqhd,bkhd->bhqk', qblk, k).astype(jnp.float32) * scale
        s_ctx = jnp.einsum('bqhd,blhd->bhql', qblk, kc).astype(jnp.float32) * scale
        p = jax.nn.softmax(jnp.concatenate([s_lat, s_ctx], axis=-1), axis=-1)
        p_lat = p[..., :N].astype(v.dtype)
        p_ctx = p[..., N:].astype(v.dtype)
        return (jnp.einsum('bhqk,bkhd->bqhd', p_lat, v)
                + jnp.einsum('bhql,blhd->bqhd', p_ctx, vc))

    out = lax.map(body, qb)
    return out.transpose(1, 0, 2, 3, 4).reshape(B, N, H * dv)


def _dense_attn(q, k, v):
    B, L, H, d = q.shape
    s = jnp.einsum('bqhd,bkhd->bhqk', q, k).astype(jnp.float32) * (d ** -0.5)
    p = jax.nn.softmax(s, axis=-1).astype(v.dtype)
    return jnp.einsum('bhqk,bkhd->bqhd', p, v).reshape(B, L, -1)


def _pool_mix(u, pool_w, pool_scale):
    B, N, _ = u.shape
    uf = u.astype(jnp.float32)
    cs = jnp.concatenate([jnp.zeros((B, 1, GROUP_W), jnp.float32), jnp.cumsum(uf, axis=1)], axis=1)
    t = jnp.arange(N)
    outs = []
    for g, w in enumerate(POOL_WINDOWS):
        lo = jnp.clip(t - w // 2, 0, N)
        hi = jnp.clip(t + w - w // 2, 0, N)
        ch = slice(g * POOL_CH, (g + 1) * POOL_CH)
        mean = (cs[:, hi, ch] - cs[:, lo, ch]) / (hi - lo).astype(jnp.float32)[None, :, None]
        outs.append(mean - uf[:, :, ch])
    pooled = jnp.stack(outs, axis=2).astype(u.dtype)
    y = jnp.einsum('bngc,gcd->bngd', pooled, pool_w).reshape(B, N, GROUP_W)
    return y * pool_scale


def _fourier_mix(u, fno_w):
    B, N, _ = u.shape
    ug = u.astype(jnp.float32).reshape(B, N, FNO_GROUPS, FNO_CH)
    f = jnp.fft.fft2(ug, axes=(1, 3), norm='ortho').real
    return f.reshape(B, N, GROUP_W).astype(u.dtype) @ fno_w


def _mla_q(q_c, q_norm, w_uq, positioned):
    B, N, _ = q_c.shape
    q = (_rmsnorm(q_c, q_norm) @ w_uq).reshape(B, N, MLA_HEADS, MLA_NOPE + MLA_ROPE)
    if positioned:
        q = jnp.concatenate([q[..., :MLA_NOPE], _rope_2d(q[..., MLA_NOPE:])], axis=-1)
    return q


def _mla_kv(kv_c, k_r, kv_norm, w_uk, w_uv, positioned):
    B, N, _ = kv_c.shape
    ckv = _rmsnorm(kv_c, kv_norm)
    k_nope = (ckv @ w_uk).reshape(B, N, MLA_HEADS, MLA_NOPE)
    v = (ckv @ w_uv).reshape(B, N, MLA_HEADS, MLA_V)
    k_rope = k_r[:, :, None, :]
    if positioned:
        k_rope = _rope_2d(k_rope)
    k = jnp.concatenate([k_nope, jnp.broadcast_to(k_rope, (B, N, MLA_HEADS, MLA_ROPE))], axis=-1)
    return k, v


def _merge(outs, grp_norm, w_out):
    cat = jnp.concatenate([_rmsnorm(o, grp_norm[g]) for g, o in enumerate(outs)], axis=-1)
    return cat @ w_out


def _mixer(hl, hc, w_in, na_bias, pool_w, pool_scale, fno_w, mla_q_norm, mla_w_uq, mla_kv_norm,
           mla_w_uk, mla_w_uv, grp_norm, w_out, with_ctx):
    cuts = np.cumsum(PROJ_SIZES)[:-1].tolist()
    qa_l, ka_l, va_l, pl_l, fl_l, qc_l, kvc_l, kr_l = jnp.split(hl @ w_in, cuts, axis=-1)
    qa_c, ka_c, va_c, pl_c, fl_c, qc_c, kvc_c, kr_c = jnp.split(hc @ w_in, cuts, axis=-1)
    kD_c, vD_c = _mla_kv(kvc_c, kr_c, mla_kv_norm, mla_w_uk, mla_w_uv, False)
    kD_l, vD_l = _mla_kv(kvc_l, kr_l, mla_kv_norm, mla_w_uk, mla_w_uv, True)
    qD_l = _mla_q(qc_l, mla_q_norm, mla_w_uq, True)
    a_l = _neighbourhood_attn(_heads(qa_l), _heads(ka_l), _heads(va_l), _heads(ka_c), _heads(va_c), na_bias)
    b_l = _pool_mix(pl_l, pool_w, pool_scale)
    c_l = _fourier_mix(fl_l, fno_w)
    d_l = _block_attn(qD_l, kD_l, vD_l, kD_c, vD_c)
    out_l = _merge((a_l, b_l, c_l, d_l), grp_norm, w_out)
    if not with_ctx:
        return out_l, None
    qD_c = _mla_q(qc_c, mla_q_norm, mla_w_uq, False)
    a_c = _dense_attn(_heads(qa_c), _heads(ka_c), _heads(va_c))
    b_c = _pool_mix(pl_c, pool_w, pool_scale)
    c_c = _fourier_mix(fl_c, fno_w)
    d_c = _dense_attn(qD_c, kD_c, vD_c)
    out_c = _merge((a_c, b_c, c_c, d_c), grp_norm, w_out)
    return out_l, out_c


def _ec_moe(h, router_w, w_gate, w_up, w_down):
    B, N, D = h.shape
    aff = jax.nn.softmax((h @ router_w).astype(jnp.float32), axis=-1)
    cap = max(1, EC_CAPACITY_FACTOR * N // N_EXPERTS)
    gates, idx = lax.top_k(jnp.swapaxes(aff, 1, 2), cap)
    bidx = jnp.arange(B)[:, None, None]
    xs = h[bidx, idx]
    hid = (jax.nn.silu(jnp.einsum('becd,edf->becf', xs, w_gate))
           * jnp.einsum('becd,edf->becf', xs, w_up))
    y = jnp.einsum('becf,efd->becd', hid, w_down) * gates[..., None].astype(h.dtype)
    return jnp.zeros_like(h).at[bidx, idx].add(y)


def setup_inputs(seed: int = 0) -> dict:
    key = jax.random.key(seed)
    ks = jax.random.split(key, 26)
    nrm = jax.random.normal
    f32 = jnp.float32
    return {
        'x': nrm(ks[0], (BATCH, SEQ, D_MODEL), f32),
        'c': nrm(ks[1], (BATCH, D_MODEL), f32),
        'ctx': nrm(ks[2], (BATCH, CTX_LEN, D_MODEL), f32),
        'c_ctx': nrm(ks[3], (D_MODEL,), f32),
        'ada_w': nrm(ks[4], (DEPTH, D_MODEL, 6 * D_MODEL), f32) * (0.5 * D_MODEL ** -0.5),
        'ada_b': nrm(ks[5], (DEPTH, 6 * D_MODEL), f32) * 0.02,
        'norm1_g': 1.0 + 0.05 * nrm(ks[6], (DEPTH, D_MODEL), f32),
        'norm2_g': 1.0 + 0.05 * nrm(ks[7], (DEPTH, D_MODEL), f32),
        'w_in': nrm(ks[8], (DEPTH, D_MODEL, D_IN), f32) * D_MODEL ** -0.5,
        'na_bias': nrm(ks[9], (DEPTH, NA_HEADS, 2 * NA_WIN_R_MAX - 1, 2 * NA_WIN_C - 1), f32) * 0.1,
        'pool_w': nrm(ks[10], (DEPTH, len(POOL_WINDOWS), POOL_CH, POOL_CH), f32) * POOL_CH ** -0.5,
        'pool_scale': 1.0 + 0.05 * nrm(ks[11], (DEPTH, GROUP_W), f32),
        'fno_w': nrm(ks[12], (DEPTH, GROUP_W, GROUP_W), f32) * GROUP_W ** -0.5,
        'mla_q_norm': 1.0 + 0.05 * nrm(ks[13], (DEPTH, MLA_Q_RANK), f32),
        'mla_w_uq': nrm(ks[14], (DEPTH, MLA_Q_RANK, MLA_HEADS * (MLA_NOPE + MLA_ROPE)), f32) * MLA_Q_RANK ** -0.5,
        'mla_kv_norm': 1.0 + 0.05 * nrm(ks[15], (DEPTH, MLA_KV_RANK), f32),
        'mla_w_uk': nrm(ks[16], (DEPTH, MLA_KV_RANK, MLA_HEADS * MLA_NOPE), f32) * MLA_KV_RANK ** -0.5,
        'mla_w_uv': nrm(ks[17], (DEPTH, MLA_KV_RANK, MLA_HEADS * MLA_V), f32) * MLA_KV_RANK ** -0.5,
        'grp_norm': 1.0 + 0.05 * nrm(ks[18], (DEPTH, N_GROUPS, GROUP_W), f32),
        'w_out': nrm(ks[19], (DEPTH, D_MIX, D_MODEL), f32) * D_MIX ** -0.5,
        'router_w': nrm(ks[20], (DEPTH, D_MODEL, N_EXPERTS), f32) * D_MODEL ** -0.5,
        'exp_w_gate': nrm(ks[21], (DEPTH, N_EXPERTS, D_MODEL, D_EXPERT), f32) * D_MODEL ** -0.5,
        'exp_w_up': nrm(ks[22], (DEPTH, N_EXPERTS, D_MODEL, D_EXPERT), f32) * D_MODEL ** -0.5,
        'exp_w_down': nrm(ks[23], (DEPTH, N_EXPERTS, D_EXPERT, D_MODEL), f32) * D_EXPERT ** -0.5,
        'final_norm': 1.0 + 0.05 * nrm(ks[24], (D_MODEL,), f32),
    }


def reference(x, c, ctx, c_ctx, ada_w, ada_b, norm1_g, norm2_g, w_in, na_bias, pool_w, pool_scale, fno_w,
              mla_q_norm, mla_w_uq, mla_kv_norm, mla_w_uk, mla_w_uv, grp_norm, w_out, router_w,
              exp_w_gate, exp_w_up, exp_w_down, final_norm):
    xl, xc = x, ctx
    for i in range(DEPTH):
        last = i == DEPTH - 1
        sh1, sc1, gt1, sh2, sc2, gt2 = [m[:, None, :] for m in _adaln(c, ada_w[i], ada_b[i])]
        csh1, csc1, cgt1, csh2, csc2, cgt2 = _adaln(c_ctx, ada_w[i], ada_b[i])
        hl = _modulate(xl, norm1_g[i], sh1, sc1)
        hc = _modulate(xc, norm1_g[i], csh1, csc1)
        mix_l, mix_c = _mixer(hl, hc, w_in[i], na_bias[i], pool_w[i], pool_scale[i], fno_w[i],
                              mla_q_norm[i], mla_w_uq[i], mla_kv_norm[i], mla_w_uk[i], mla_w_uv[i],
                              grp_norm[i], w_out[i], not last)
        xl = xl + gt1 * mix_l
        xl = xl + gt2 * _ec_moe(_modulate(xl, norm2_g[i], sh2, sc2), router_w[i],
                                exp_w_gate[i], exp_w_up[i], exp_w_down[i])
        if not last:
            xc = xc + cgt1 * mix_c
            xc = xc + cgt2 * _ec_moe(_modulate(xc, norm2_g[i], csh2, csc2), router_w[i],
                                     exp_w_gate[i], exp_w_up[i], exp_w_down[i])
    return _rmsnorm(xl, final_norm)
```

```python
import functools
import math

import numpy as np
import jax
import jax.numpy as jnp
from jax import lax
from jax.experimental import pallas as pl
from jax.experimental.pallas import tpu as pltpu

F32 = jnp.float32
BF16 = jnp.bfloat16
I32 = jnp.int32

D_MODEL = 1024
GRID_W = 64
GROUP_W = 256
NA_HEADS = 4
NA_WIN_R = 8
NA_WIN_C = 16
POOL_WINDOWS = (2, 4, 8, 16)
MLA_HEADS = 4
MLA_NOPE = 64
MLA_ROPE = 32
MLA_V = 64
MLA_Q_RANK = 256
MLA_KV_RANK = 128
ROPE_BASE = 10000.0
N_EXPERTS = 16
EC_CAPACITY_FACTOR = 2
D_EXPERT = 1024
EPS = 1e-6
NEG = -1e30

LANES = 128
SUBLANES = 8
TOKEN_BLOCK = 256
VMEM_LIMIT = 56 * 1024 * 1024


def _cparams(sem, vmem=None):
    return pltpu.CompilerParams(dimension_semantics=sem, vmem_limit_bytes=vmem)


def _dot(a, b):
    return jnp.dot(a, b, preferred_element_type=F32)


def _dot_nt(a, b):
    return lax.dot_general(a, b, (((1,), (1,)), ((), ())), preferred_element_type=F32)


def _dot_tn(a, b):
    return lax.dot_general(a, b, (((0,), (0,)), ((), ())), preferred_element_type=F32)


def _split(a):
    hi = a.astype(BF16)
    lo = (a - hi.astype(F32)).astype(BF16)
    return hi, lo


def _dot3(a, b):
    a_hi, a_lo = _split(a)
    b_hi, b_lo = _split(b)
    return _dot(a_hi, b_hi) + _dot(a_lo, b_hi) + _dot(a_hi, b_lo)


def _rms(x, g):
    return x * lax.rsqrt(jnp.mean(x * x, axis=-1, keepdims=True) + EPS) * g


def _silu(x):
    return x * jax.nn.sigmoid(x)


def _ada_kernel(cs_ref, w_ref, b_ref, o_ref):
    o_ref[...] = _dot3(_silu(cs_ref[...]), w_ref[...]) + b_ref[...]


def _adaln(cs, ada_w, ada_b):
    depth, d, n6 = ada_w.shape
    tn = 1536
    return pl.pallas_call(
        _ada_kernel,
        out_shape=jax.ShapeDtypeStruct((depth, 8, n6), F32),
        grid=(depth, n6 // tn),
        in_specs=[pl.BlockSpec((8, d), lambda l, j: (0, 0)),
                  pl.BlockSpec((None, d, tn), lambda l, j: (l, 0, j)),
                  pl.BlockSpec((None, 1, tn), lambda l, j: (l, 0, j))],
        out_specs=pl.BlockSpec((None, 8, tn), lambda l, j: (l, 0, j)),
        compiler_params=_cparams(("arbitrary", "arbitrary"), VMEM_LIMIT),
    )(cs, ada_w, ada_b.reshape(depth, 1, n6))


_C_QA, _C_KA, _C_VA, _C_PL, _C_FL, _C_QC, _C_KVC, _C_KR, _C_KRS, _C_END = (
    0, 256, 512, 768, 1024, 1280, 1536, 1664, 1792, 1920)


def _in_kernel(x_ref, sh_ref, sc_ref, g_ref, w_ref, qn_ref, kvn_ref, wq_ref, cos_ref, sin_ref,
               qa_ref, ka_ref, va_ref, pl_ref, fl_ref, q_ref, kt_ref, v_ref):
    x = x_ref[...]
    h = _rms(x, g_ref[...]) * (1.0 + sc_ref[...]) + sh_ref[...]
    proj = _dot(h.astype(BF16), w_ref[...])
    qa_ref[...] = (proj[:, _C_QA:_C_KA] * (GROUP_W // NA_HEADS) ** -0.5).astype(BF16)
    ka_ref[...] = proj[:, _C_KA:_C_VA].astype(BF16)
    va_ref[...] = proj[:, _C_VA:_C_PL].astype(BF16)
    pl_ref[...] = proj[:, _C_PL:_C_FL]
    fl_ref[...] = proj[:, _C_FL:_C_QC].astype(BF16)
    cos = cos_ref[...]
    sin = sin_ref[...]
    qn = _rms(proj[:, _C_QC:_C_KVC], qn_ref[...]).astype(BF16)
    qf = _dot(qn, wq_ref[...])
    qrope = qf[:, 512:640] * cos + qf[:, 640:768] * sin
    head_of_lane = lax.broadcasted_iota(I32, (1, LANES), 1) // MLA_ROPE
    for hd in range(MLA_HEADS):
        rope_h = jnp.where(head_of_lane == hd, qrope, 0.0)
        q_ref[hd] = jnp.concatenate([qf[:, hd * 128:(hd + 1) * 128], rope_h], axis=-1).astype(BF16)
    ckv = _rms(proj[:, _C_KVC:_C_KR], kvn_ref[...])
    krope = proj[:, _C_KR:_C_KRS] * cos + proj[:, _C_KRS:_C_END] * sin
    kt_ref[...] = jnp.concatenate([ckv, krope], axis=-1).T.astype(BF16)
    v_ref[...] = ckv.astype(BF16)


def _in_proj(x, shift, scale, g, w_all, qn, kvn, wq_all, cos4, sin4, tm):
    b, n, d = x.shape
    grid = (b, n // tm)
    row = lambda bi, i: (bi, i, 0)
    vec = lambda bi, i: (bi, 0, 0)
    const = lambda bi, i: (0, 0)
    outs = [jax.ShapeDtypeStruct((b, n, GROUP_W), BF16)] * 3 + [
        jax.ShapeDtypeStruct((b, n, GROUP_W), F32),
        jax.ShapeDtypeStruct((b, n, GROUP_W), BF16),
        jax.ShapeDtypeStruct((b, MLA_HEADS, n, 256), BF16),
        jax.ShapeDtypeStruct((b, 256, n), BF16),
        jax.ShapeDtypeStruct((b, n, MLA_KV_RANK), BF16)]
    return pl.pallas_call(
        _in_kernel,
        out_shape=outs,
        grid=grid,
        in_specs=[pl.BlockSpec((None, tm, d), row),
                  pl.BlockSpec((None, 1, d), vec),
                  pl.BlockSpec((None, 1, d), vec),
                  pl.BlockSpec((1, d), const),
                  pl.BlockSpec((d, _C_END), const),
                  pl.BlockSpec((1, MLA_Q_RANK), const),
                  pl.BlockSpec((1, MLA_KV_RANK), const),
                  pl.BlockSpec((MLA_Q_RANK, 768), const),
                  pl.BlockSpec((tm, LANES), lambda bi, i: (i, 0)),
                  pl.BlockSpec((tm, LANES), lambda bi, i: (i, 0))],
        out_specs=[pl.BlockSpec((None, tm, GROUP_W), row)] * 5 + [
            pl.BlockSpec((None, MLA_HEADS, tm, 256), lambda bi, i: (bi, 0, i, 0)),
            pl.BlockSpec((None, 256, tm), lambda bi, i: (bi, 0, i)),
            pl.BlockSpec((None, tm, MLA_KV_RANK), row)],
        compiler_params=_cparams(("parallel", "arbitrary"), VMEM_LIMIT),
    )(x, shift, scale, g, w_all, qn, kvn, wq_all, cos4, sin4)


_NA_ROWS_PER_STEP = 8


def _masked_head_attn(q, blocks, head_of_lane):
    acc = jnp.zeros((q.shape[0], GROUP_W), F32)
    for hd in range(NA_HEADS):
        sel = head_of_lane == hd
        qh = jnp.where(sel, q, jnp.zeros_like(q))
        ss = []
        for k, _, bias in blocks:
            s = _dot_nt(qh, k)
            ss.append(s if bias is None else s + bias[hd])
        m = ss[0].max(axis=-1, keepdims=True)
        for s in ss[1:]:
            m = jnp.maximum(m, s.max(axis=-1, keepdims=True))
        l = jnp.zeros_like(m)
        o = jnp.zeros((q.shape[0], GROUP_W), F32)
        for s, (_, v, _) in zip(ss, blocks):
            p = jnp.exp(s - m)
            l = l + p.sum(axis=-1, keepdims=True)
            o = o + _dot(p.astype(BF16), v)
        acc = acc + jnp.where(sel, o / l, 0.0)
    return acc


def _na_kernel(q_ref, k_ref, v_ref, kc_ref, vc_ref, bm_ref, o_ref, *, rows):
    i = pl.program_id(1)
    head_of_lane = lax.broadcasted_iota(I32, (1, GROUP_W), 1) // (GROUP_W // NA_HEADS)
    kc = kc_ref[...]
    vc = vc_ref[...]
    win = NA_WIN_R * GRID_W

    def body(r, carry):
        qy = i * _NA_ROWS_PER_STEP + r
        rs = jnp.clip(qy - NA_WIN_R // 2, 0, rows - NA_WIN_R)
        d = qy - rs
        q0 = pl.multiple_of(r * GRID_W, GRID_W)
        k0 = pl.multiple_of(rs * GRID_W, GRID_W)
        q = q_ref[pl.ds(q0, GRID_W), :]
        kw = k_ref[pl.ds(k0, win), :]
        vw = v_ref[pl.ds(k0, win), :]
        bias = bm_ref[d]
        o_ref[pl.ds(q0, GRID_W), :] = _masked_head_attn(
            q, [(kw, vw, bias), (kc, vc, None)], head_of_lane)
        return carry

    lax.fori_loop(0, _NA_ROWS_PER_STEP, body, 0)


def _na_attn(qa, ka, va, kc, vc, bm):
    b, n, _ = qa.shape
    rows = n // GRID_W
    lc = kc.shape[1]
    tq = _NA_ROWS_PER_STEP * GRID_W
    full = lambda bi, i: (bi, 0, 0)
    return pl.pallas_call(
        functools.partial(_na_kernel, rows=rows),
        out_shape=jax.ShapeDtypeStruct((b, n, GROUP_W), F32),
        grid=(b, n // tq),
        in_specs=[pl.BlockSpec((None, tq, GROUP_W), lambda bi, i: (bi, i, 0)),
                  pl.BlockSpec((None, n, GROUP_W), full),
                  pl.BlockSpec((None, n, GROUP_W), full),
                  pl.BlockSpec((None, lc, GROUP_W), full),
                  pl.BlockSpec((None, lc, GROUP_W), full),
                  pl.BlockSpec(bm.shape, lambda bi, i: (0, 0, 0, 0))],
        out_specs=pl.BlockSpec((None, tq, GROUP_W), lambda bi, i: (bi, i, 0)),
        compiler_params=_cparams(("parallel", "arbitrary"), VMEM_LIMIT),
    )(qa, ka, va, kc, vc, bm)


def _dense_kernel(q_ref, k_ref, v_ref, o_ref):
    head_of_lane = lax.broadcasted_iota(I32, (1, GROUP_W), 1) // (GROUP_W // NA_HEADS)
    o_ref[...] = _masked_head_attn(q_ref[...], [(k_ref[...], v_ref[...], None)], head_of_lane)


def _dense_attn(q, k, v):
    b, n, _ = q.shape
    spec = pl.BlockSpec((None, n, GROUP_W), lambda bi: (bi, 0, 0))
    return pl.pallas_call(
        _dense_kernel,
        out_shape=jax.ShapeDtypeStruct((b, n, GROUP_W), F32),
        grid=(b,),
        in_specs=[spec, spec, spec],
        out_specs=spec,
        compiler_params=_cparams(("parallel",)),
    )(q, k, v)


def _na_bias_mask(na_bias, rows):
    win_r = min(NA_WIN_R, rows)
    d = np.arange(win_r)[:, None, None, None]
    qx = np.arange(GRID_W)[None, :, None, None]
    r = np.arange(win_r)[None, None, :, None]
    col = np.arange(GRID_W)[None, None, None, :]
    cs = np.clip(qx - NA_WIN_C // 2, 0, GRID_W - NA_WIN_C)
    valid = (col >= cs) & (col < cs + NA_WIN_C)
    dr = np.broadcast_to(r - d + (NA_WIN_R - 1), (win_r, GRID_W, win_r, GRID_W))
    dc = np.broadcast_to(np.clip(col - qx + (NA_WIN_C - 1), 0, 2 * NA_WIN_C - 2),
                         (win_r, GRID_W, win_r, GRID_W))
    valid = np.broadcast_to(valid, dr.shape)
    tab = na_bias[:, dr, dc]
    tab = jnp.where(valid[None], tab, NEG)
    return tab.transpose(1, 0, 2, 3, 4).reshape(win_r, NA_HEADS, GRID_W, win_r * GRID_W)


_HALO = 8


def _pool_kernel(u_ref, up_ref, un_ref, w_ref, sc_ref, o_ref, ext_ref, *, tm, n):
    i = pl.program_id(1)
    nt = pl.num_programs(1)
    u = u_ref[...]
    ext_ref[0:_HALO, :] = jnp.where(i > 0, up_ref[...], 0.0)
    ext_ref[_HALO:_HALO + tm, :] = u
    ext_ref[_HALO + tm:2 * _HALO + tm, :] = jnp.where(i < nt - 1, un_ref[...], 0.0)

    def sh(o):
        return ext_ref[_HALO + o:_HALO + o + tm, :]

    t = i * tm + lax.broadcasted_iota(I32, (tm, 1), 0)
    group = lax.broadcasted_iota(I32, (1, GROUP_W), 1) // (GROUP_W // len(POOL_WINDOWS))
    s = jnp.zeros_like(u)
    mean = jnp.zeros_like(u)
    lo_prev, hi_prev = 0, 0
    for g, w in enumerate(POOL_WINDOWS):
        lo, hi = -(w // 2), w - w // 2
        for o in list(range(lo, lo_prev)) + list(range(hi_prev, hi)):
            s = s + sh(o)
        lo_prev, hi_prev = lo, hi
        cnt = (jnp.minimum(t + hi, n) - jnp.maximum(t + lo, 0)).astype(F32)
        mean = jnp.where(group == g, s / cnt, mean)
    pooled = (mean - u).astype(BF16)
    o_ref[...] = _dot(pooled, w_ref[...]) * sc_ref[...]


def _pool_mix(u, w_bd, scale, tm):
    b, n, _ = u.shape
    hb = tm // _HALO
    nh = n // _HALO
    return pl.pallas_call(
        functools.partial(_pool_kernel, tm=tm, n=n),
        out_shape=jax.ShapeDtypeStruct((b, n, GROUP_W), F32),
        grid=(b, n // tm),
        in_specs=[pl.BlockSpec((None, tm, GROUP_W), lambda bi, i: (bi, i, 0)),
                  pl.BlockSpec((None, _HALO, GROUP_W), lambda bi, i: (bi, jnp.maximum(i * hb - 1, 0), 0)),
                  pl.BlockSpec((None, _HALO, GROUP_W), lambda bi, i: (bi, jnp.minimum((i + 1) * hb, nh - 1), 0)),
                  pl.BlockSpec((GROUP_W, GROUP_W), lambda bi, i: (0, 0)),
                  pl.BlockSpec((1, GROUP_W), lambda bi, i: (0, 0))],
        out_specs=pl.BlockSpec((None, tm, GROUP_W), lambda bi, i: (bi, i, 0)),
        scratch_shapes=[pltpu.VMEM((tm + 2 * _HALO, GROUP_W), F32)],
        compiler_params=_cparams(("parallel", "arbitrary")),
    )(u, u, u, w_bd, scale)


_FFT_R = 128
_FFT_C = 64
_FFT_CB = 8
_FFT_KB = 16


def _fft_a_kernel(u_ref, m_ref, z_ref):
    for j in range(_FFT_CB):
        sl = slice(j * GROUP_W, (j + 1) * GROUP_W)
        z = _dot(m_ref[j].astype(BF16), u_ref[:, sl])
        z_ref[0, :, sl] = z[:_FFT_R].astype(BF16)
        z_ref[1, :, sl] = z[_FFT_R:].astype(BF16)


def _fft_b_kernel(z_ref, g_ref, ab_ref, o_ref):
    g = g_ref[...].astype(BF16)
    for j in range(_FFT_KB):
        slab = jnp.concatenate([z_ref[0, j], z_ref[1, j]], axis=0)
        w = _dot(g, slab)
        lhs = jnp.concatenate([w[:_FFT_C], w[_FFT_C:]], axis=1).astype(BF16)
        o_ref[:, j, :] = _dot(lhs, ab_ref[...])


def _fft_tables():
    n = _FFT_R * _FFT_C
    k1 = np.arange(_FFT_R, dtype=np.float64)[None, :, None]
    r = np.arange(_FFT_R, dtype=np.float64)[None, None, :]
    c = np.arange(_FFT_C, dtype=np.float64)[:, None, None]
    ang = 2.0 * np.pi * np.mod(k1 * (_FFT_C * r + c), n) / n
    m = np.concatenate([np.cos(ang), -np.sin(ang)], axis=1)
    k2 = np.arange(_FFT_C, dtype=np.float64)[:, None]
    cc = np.arange(_FFT_C, dtype=np.float64)[None, :]
    a2 = 2.0 * np.pi * np.mod(k2 * cc, _FFT_C) / _FFT_C
    g = np.block([[np.cos(a2), np.sin(a2)], [-np.sin(a2), np.cos(a2)]])
    return jnp.asarray(m, F32), jnp.asarray(g, F32)


def _channel_dft(fno_w, n_pos):
    ch = GROUP_W // 4
    i = np.arange(GROUP_W)
    same = (i[:, None] // ch) == (i[None, :] // ch)
    ang = 2.0 * np.pi * np.mod((i[:, None] % ch) * (i[None, :] % ch), ch) / ch
    norm = 1.0 / math.sqrt(n_pos * ch)
    cc = jnp.asarray(np.where(same, np.cos(ang), 0.0) * norm, F32)
    sc = jnp.asarray(np.where(same, np.sin(ang), 0.0) * norm, F32)
    hp = lax.Precision.HIGHEST
    return jnp.concatenate([jnp.dot(cc, fno_w, precision=hp), jnp.dot(sc, fno_w, precision=hp)], axis=0).astype(BF16)


def _fourier_mix(fl, fno_w):
    b, n, _ = fl.shape
    assert n == _FFT_R * _FFT_C
    m_tab, g_tab = _fft_tables()
    ab = _channel_dft(fno_w, n)
    u2 = fl.reshape(b, _FFT_R, _FFT_C * GROUP_W)
    cw = _FFT_CB * GROUP_W
    z = pl.pallas_call(
        _fft_a_kernel,
        out_shape=jax.ShapeDtypeStruct((b, 2, _FFT_R, _FFT_C * GROUP_W), BF16),
        grid=(b, _FFT_C // _FFT_CB),
        in_specs=[pl.BlockSpec((None, _FFT_R, cw), lambda bi, i: (bi, 0, i)),
                  pl.BlockSpec((_FFT_CB, 2 * _FFT_R, _FFT_R), lambda bi, i: (i, 0, 0))],
        out_specs=pl.BlockSpec((None, 2, _FFT_R, cw), lambda bi, i: (bi, 0, 0, i)),
        compiler_params=_cparams(("parallel", "arbitrary")),
    )(u2, m_tab)
    z5 = z.reshape(b, 2, _FFT_R, _FFT_C, GROUP_W)
    out = pl.pallas_call(
        _fft_b_kernel,
        out_shape=jax.ShapeDtypeStruct((b, _FFT_C, _FFT_R, GROUP_W), F32),
        grid=(b, _FFT_R // _FFT_KB),
        in_specs=[pl.BlockSpec((None, 2, _FFT_KB, _FFT_C, GROUP_W), lambda bi, i: (bi, 0, i, 0, 0)),
                  pl.BlockSpec((2 * _FFT_C, 2 * _FFT_C), lambda bi, i: (0, 0)),
                  pl.BlockSpec((2 * GROUP_W, GROUP_W), lambda bi, i: (0, 0))],
        out_specs=pl.BlockSpec((None, _FFT_C, _FFT_KB, GROUP_W), lambda bi, i: (bi, 0, i, 0)),
        compiler_params=_cparams(("parallel", "arbitrary")),
    )(z5, g_tab, ab)
    return out.reshape(b, n, GROUP_W)


def _fft_small_kernel(u_ref, m_ref, ab_ref, o_ref, *, n):
    w = _dot(m_ref[...].astype(BF16), u_ref[...])
    lhs = jnp.concatenate([w[:n], w[n:]], axis=1).astype(BF16)
    o_ref[...] = _dot(lhs, ab_ref[...])


def _fourier_mix_small(fl, fno_w):
    b, n, _ = fl.shape
    k = np.arange(n, dtype=np.float64)
    ang = 2.0 * np.pi * np.mod(k[:, None] * k[None, :], n) / n
    m = jnp.asarray(np.concatenate([np.cos(ang), -np.sin(ang)], axis=0), F32)
    ab = _channel_dft(fno_w, n)
    return pl.pallas_call(
        functools.partial(_fft_small_kernel, n=n),
        out_shape=jax.ShapeDtypeStruct((b, n, GROUP_W), F32),
        grid=(b,),
        in_specs=[pl.BlockSpec((None, n, GROUP_W), lambda bi: (bi, 0, 0)),
                  pl.BlockSpec((2 * n, n), lambda bi: (0, 0)),
                  pl.BlockSpec((2 * GROUP_W, GROUP_W), lambda bi: (0, 0))],
        out_specs=pl.BlockSpec((None, n, GROUP_W), lambda bi: (bi, 0, 0)),
        compiler_params=_cparams(("parallel",)),
    )(fl, m, ab)


def _flash_kernel(q_ref, kt_ref, v_ref, wuv_ref, o_ref, m_sc, l_sc, acc_sc, *, tq, tk, nkt):
    q = q_ref[...].reshape(MLA_HEADS * tq, 256)
    m_sc[...] = jnp.full_like(m_sc, NEG)
    l_sc[...] = jnp.zeros_like(l_sc)
    acc_sc[...] = jnp.zeros_like(acc_sc)

    def body(j, carry):
        off = pl.multiple_of(j * tk, tk)
        s = _dot(q, kt_ref[:, pl.ds(off, tk)])
        m_prev = m_sc[...]
        m_new = jnp.maximum(m_prev, s.max(axis=-1, keepdims=True))
        alpha = jnp.exp(m_prev - m_new)
        p = jnp.exp(s - m_new)
        l_sc[...] = alpha * l_sc[...] + p.sum(axis=-1, keepdims=True)
        acc_sc[...] = alpha * acc_sc[...] + _dot(p.astype(BF16), v_ref[pl.ds(off, tk), :])
        m_sc[...] = m_new
        return carry

    lax.fori_loop(0, nkt, body, 0)
    o = (acc_sc[...] / l_sc[...]).astype(BF16)
    out = jnp.zeros((tq, GROUP_W), F32)
    for hd in range(MLA_HEADS):
        out = out + _dot(o[hd * tq:(hd + 1) * tq], wuv_ref[hd])
    o_ref[...] = out


def _latent_attn(q, kt, v, wuv_pad, tq, tk):
    b, _, n, _ = q.shape
    nk = kt.shape[2]
    assert nk % tk == 0
    return pl.pallas_call(
        functools.partial(_flash_kernel, tq=tq, tk=tk, nkt=nk // tk),
        out_shape=jax.ShapeDtypeStruct((b, n, GROUP_W), F32),
        grid=(b, n // tq),
        in_specs=[pl.BlockSpec((None, MLA_HEADS, tq, 256), lambda bi, i: (bi, 0, i, 0)),
                  pl.BlockSpec((None, 256, nk), lambda bi, i: (bi, 0, 0)),
                  pl.BlockSpec((None, nk, MLA_KV_RANK), lambda bi, i: (bi, 0, 0)),
                  pl.BlockSpec((MLA_HEADS, MLA_KV_RANK, GROUP_W), lambda bi, i: (0, 0, 0))],
        out_specs=pl.BlockSpec((None, tq, GROUP_W), lambda bi, i: (bi, i, 0)),
        scratch_shapes=[pltpu.VMEM((MLA_HEADS * tq, 1), F32),
                        pltpu.VMEM((MLA_HEADS * tq, 1), F32),
                        pltpu.VMEM((MLA_HEADS * tq, MLA_KV_RANK), F32)],
        compiler_params=_cparams(("parallel", "arbitrary"), VMEM_LIMIT),
    )(q, kt, v, wuv_pad)


def _merge_kernel(xl_ref, a_ref, b_ref, c_ref, d_ref, gn_ref, wout_ref, gt1_ref, g2_ref, sh2_ref, sc2_ref,
                  rw_ref, xm_ref, h2_ref, afft_ref, wout_sc):
    @pl.when((pl.program_id(0) == 0) & (pl.program_id(1) == 0))
    def _():
        wout_sc[...] = wout_ref[...].astype(BF16)

    mix = jnp.zeros(xl_ref.shape, F32)
    for g, ref in enumerate((a_ref, b_ref, c_ref, d_ref)):
        mix = mix + _dot(_rms(ref[...], gn_ref[g]).astype(BF16), wout_sc[g * GROUP_W:(g + 1) * GROUP_W, :])
    xm = xl_ref[...] + gt1_ref[...] * mix
    xm_ref[...] = xm
    h2 = _rms(xm, g2_ref[...]) * (1.0 + sc2_ref[...]) + sh2_ref[...]
    h2_ref[...] = h2.astype(BF16)
    logits = _dot3(h2, rw_ref[...])
    lane = lax.broadcasted_iota(I32, (1, LANES), 1)
    logits = jnp.where(lane < N_EXPERTS, logits, NEG)
    e = jnp.exp(logits - logits.max(axis=-1, keepdims=True))
    aff = e / e.sum(axis=-1, keepdims=True)
    afft_ref[...] = aff.T[:N_EXPERTS, :]


def _merge_router(xl, a, bb, c, d, gn, w_out, gt1, g2, sh2, sc2, rw_pad, tm):
    b, n, dm = xl.shape
    row = lambda bi, i: (bi, i, 0)
    vec = lambda bi, i: (bi, 0, 0)
    const2 = lambda bi, i: (0, 0)
    grp = pl.BlockSpec((None, tm, GROUP_W), row)
    return pl.pallas_call(
        _merge_kernel,
        out_shape=[jax.ShapeDtypeStruct((b, n, dm), F32),
                   jax.ShapeDtypeStruct((b, n, dm), BF16),
                   jax.ShapeDtypeStruct((b, N_EXPERTS, n), F32)],
        grid=(b, n // tm),
        in_specs=[pl.BlockSpec((None, tm, dm), row), grp, grp, grp, grp,
                  pl.BlockSpec((4, 1, GROUP_W), lambda bi, i: (0, 0, 0)),
                  pl.BlockSpec((dm, dm), const2),
                  pl.BlockSpec((None, 1, dm), vec),
                  pl.BlockSpec((1, dm), const2),
                  pl.BlockSpec((None, 1, dm), vec),
                  pl.BlockSpec((None, 1, dm), vec),
                  pl.BlockSpec((dm, LANES), const2)],
        out_specs=[pl.BlockSpec((None, tm, dm), row),
                   pl.BlockSpec((None, tm, dm), row),
                   pl.BlockSpec((None, N_EXPERTS, tm), lambda bi, i: (bi, 0, i))],
        scratch_shapes=[pltpu.VMEM((dm, dm), BF16)],
        compiler_params=_cparams(("arbitrary", "arbitrary"), VMEM_LIMIT),
    )(xl, a, bb, c, d, gn, w_out, gt1, g2, sh2, sc2, rw_pad)


def _select_kernel(a_ref, u_ref, ones_ref, lblk_ref, same_ref, slot_ref, cumx_ref, *, cap):
    rows = a_ref.shape[0] * a_ref.shape[1]
    x = a_ref[...].reshape(rows, LANES)
    ones = ones_ref[...]
    same = same_ref[...]

    def expert_total(mask):
        rt = _dot(mask.astype(BF16), ones)
        return _dot(same, rt.astype(BF16))

    def cumsum(mask):
        mb = mask.astype(BF16)
        rt = _dot(mb, ones)
        return _dot(mb, u_ref[...]) + _dot(lblk_ref[...], rt.astype(BF16))

    def body(i, t):
        cand = t | jnp.left_shift(jnp.int32(1), 30 - i)
        cnt = expert_total(jnp.where(x >= pltpu.bitcast(cand, F32), 1.0, 0.0))
        return jnp.where(cnt >= cap, cand, t)

    thr_bits = lax.fori_loop(0, 31, body, jnp.zeros((rows, LANES), I32))
    thr = pltpu.bitcast(thr_bits, F32)
    gt = jnp.where(x > thr, 1.0, 0.0)
    eq = jnp.where(x == thr, 1.0, 0.0)
    need = cap - expert_total(gt)
    sel = gt + eq * jnp.where(cumsum(eq) <= need, 1.0, 0.0)
    cumx = cumsum(sel) - sel
    shape = a_ref.shape
    cumx_i = cumx.astype(I32)
    slot_ref[...] = jnp.where(sel > 0.5, cumx_i, -1).reshape(shape)
    cumx_ref[...] = cumx_i.reshape(shape)


def _select(aff_t, cap):
    b, e, n = aff_t.shape
    nc = n // LANES
    rows = e * nc
    u = jnp.asarray(np.triu(np.ones((LANES, LANES))), BF16)
    ones = jnp.ones((LANES, LANES), BF16)
    r = np.arange(rows)
    same_e = (r[:, None] // nc) == (r[None, :] // nc)
    lblk = jnp.asarray(same_e & (r[None, :] < r[:, None]), BF16)
    same = jnp.asarray(same_e, BF16)
    blk = pl.BlockSpec((None, e, nc, LANES), lambda bi: (bi, 0, 0, 0))
    sq = pl.BlockSpec((LANES, LANES), lambda bi: (0, 0))
    rr = pl.BlockSpec((rows, rows), lambda bi: (0, 0))
    slot, cumx = pl.pallas_call(
        functools.partial(_select_kernel, cap=cap),
        out_shape=[jax.ShapeDtypeStruct((b, e, nc, LANES), I32)] * 2,
        grid=(b,),
        in_specs=[blk, sq, sq, rr, rr],
        out_specs=[blk, blk],
        compiler_params=_cparams(("parallel",), VMEM_LIMIT),
    )(aff_t.reshape(b, e, nc, LANES), u, ones, lblk, same)
    return slot, cumx


def _onehot_rows(sl, starts, w, vals=None):
    j = lax.broadcasted_iota(I32, (w, 1), 0)
    rows = [jnp.where(sl[e:e + 1, :] == starts[e] + j,
                      jnp.float32(1.0) if vals is None else vals[e:e + 1, :].astype(F32), jnp.float32(0.0))
            for e in range(N_EXPERTS)]
    return jnp.concatenate(rows, axis=0).astype(BF16)


def _rows_to_tiles(ref, val):
    for s in range(SUBLANES):
        ref[:, s, :] = val[:, s * LANES:(s + 1) * LANES]


def _tiles_to_rows(ref):
    return jnp.concatenate([ref[:, s, :] for s in range(SUBLANES)], axis=1)


def _dispatch_kernel(s0_ref, np_ref, h2_ref, sl_ref, xs_ref, buf, sem, pend, *, nb, w, cap):
    b = pl.program_id(0)
    i = pl.program_id(1)
    first = (b == 0) & (i == 0)
    last = (b == pl.num_programs(0) - 1) & (i == nb - 1)

    @pl.when(first)
    def _():
        pend[0] = 0

    rhs = h2_ref[...]
    sl = sl_ref[...]

    def copies(starts):
        return [pltpu.make_async_copy(buf.at[pl.ds(e * w, w)],
                                      xs_ref.at[b, e, pl.ds(starts[e], w)], sem.at[0])
                for e in range(N_EXPERTS)]

    def drain():
        @pl.when(pend[0] == 1)
        def _():
            for cp in copies([0] * N_EXPERTS):
                cp.wait()
            pend[0] = 0

    def do_pass(p, carry):
        starts = [jnp.minimum(s0_ref[(b * N_EXPERTS + e) * nb + i] + p * w, cap) for e in range(N_EXPERTS)]
        win = _dot(_onehot_rows(sl, starts, w), rhs)
        drain()
        _rows_to_tiles(buf, win)
        for cp in copies(starts):
            cp.start()
        pend[0] = 1
        return carry

    lax.fori_loop(0, np_ref[b * nb + i], do_pass, 0)

    @pl.when(i == nb - 1)
    def _():
        drain()
        buf[...] = jnp.zeros_like(buf)
        for cp in copies([cap] * N_EXPERTS):
            cp.start()
        pend[0] = 1

    @pl.when(last)
    def _():
        drain()


def _dispatch(h2, slot_en, s0, npass, cap, w):
    b, n, dm = h2.shape
    assert dm == SUBLANES * LANES
    nb = n // TOKEN_BLOCK
    kern = functools.partial(_dispatch_kernel, nb=nb, w=w, cap=cap)
    return pl.pallas_call(
        kern,
        out_shape=jax.ShapeDtypeStruct((b, N_EXPERTS, cap + w, SUBLANES, LANES), F32),
        grid_spec=pltpu.PrefetchScalarGridSpec(
            num_scalar_prefetch=2,
            grid=(b, nb),
            in_specs=[pl.BlockSpec((None, TOKEN_BLOCK, dm), lambda bi, i, *_: (bi, i, 0)),
                      pl.BlockSpec((None, N_EXPERTS, TOKEN_BLOCK), lambda bi, i, *_: (bi, 0, i))],
            out_specs=pl.BlockSpec(memory_space=pl.ANY),
            scratch_shapes=[pltpu.VMEM((N_EXPERTS * w, SUBLANES, LANES), F32),
                            pltpu.SemaphoreType.DMA((1,)),
                            pltpu.SMEM((1,), I32)]),
        compiler_params=_cparams(("arbitrary", "arbitrary"), VMEM_LIMIT),
    )(s0, npass, h2, slot_en)


def _ffn_kernel(xs_ref, wg_ref, wu_ref, wd_ref, y_ref, wg_sc, wu_sc, wd_sc):
    @pl.when((pl.program_id(1) == 0) & (pl.program_id(2) == 0))
    def _():
        wg_sc[...] = wg_ref[...].astype(BF16)
        wu_sc[...] = wu_ref[...].astype(BF16)
        wd_sc[...] = wd_ref[...].astype(BF16)

    xb = _tiles_to_rows(xs_ref).astype(BF16)
    hid = _silu(_dot(xb, wg_sc[...])) * _dot(xb, wu_sc[...])
    _rows_to_tiles(y_ref, _dot(hid.astype(BF16), wd_sc[...]))


def _expert_ffn(xs, w_gate, w_up, w_down, cap, tmf):
    b = xs.shape[0]
    dm, de = w_gate.shape[1], w_gate.shape[2]
    wspec = lambda r, c: pl.BlockSpec((None, r, c), lambda e, bi, t: (e, 0, 0))
    tile = pl.BlockSpec((None, None, tmf, SUBLANES, LANES), lambda e, bi, t: (bi, e, t, 0, 0))
    return pl.pallas_call(
        _ffn_kernel,
        out_shape=jax.ShapeDtypeStruct((b, N_EXPERTS, cap, SUBLANES, LANES), F32),
        grid=(N_EXPERTS, b, cap // tmf),
        in_specs=[tile, wspec(dm, de), wspec(dm, de), wspec(de, dm)],
        out_specs=tile,
        scratch_shapes=[pltpu.VMEM((dm, de), BF16), pltpu.VMEM((dm, de), BF16), pltpu.VMEM((de, dm), BF16)],
        compiler_params=_cparams(("arbitrary", "arbitrary", "arbitrary"), VMEM_LIMIT),
    )(xs, w_gate, w_up, w_down)


def _combine_kernel(s0_ref, np_ref, y_ref, sl_ref, aff_ref, xm_ref, gt2_ref, fn_ref, o_ref, ybuf, acc_sc, sem,
                    *, nb, w, cap, final):
    b = pl.program_id(0)
    i = pl.program_id(1)
    sl = sl_ref[...]
    g_hi, g_lo = _split(aff_ref[...])
    acc_sc[...] = jnp.zeros_like(acc_sc)

    def do_pass(p, carry):
        starts = [jnp.minimum(s0_ref[(b * N_EXPERTS + e) * nb + i] + p * w, cap - w) for e in range(N_EXPERTS)]
        cps = [pltpu.make_async_copy(y_ref.at[b, e, pl.ds(starts[e], w)],
                                     ybuf.at[pl.ds(e * w, w)], sem.at[0]) for e in range(N_EXPERTS)]
        for cp in cps:
            cp.start()
        w_hi = _onehot_rows(sl, starts, w, g_hi)
        w_lo = _onehot_rows(sl, starts, w, g_lo)
        for cp in cps:
            cp.wait()
        y_hi, y_lo = _split(_tiles_to_rows(ybuf))
        acc_sc[...] += _dot_tn(w_hi, y_hi) + _dot_tn(w_hi, y_lo) + _dot_tn(w_lo, y_hi)
        return carry

    lax.fori_loop(0, np_ref[b * nb + i], do_pass, 0)
    xo = xm_ref[...] + gt2_ref[...] * acc_sc[...]
    o_ref[...] = _rms(xo, fn_ref[...]) if final else xo


def _combine(y, slot_en, aff_t, xm, gt2, fn, s0, npass, cap, w, final):
    b, n, dm = xm.shape
    nb = n // TOKEN_BLOCK
    kern = functools.partial(_combine_kernel, nb=nb, w=w, cap=cap, final=final)
    per_expert = pl.BlockSpec((None, N_EXPERTS, TOKEN_BLOCK), lambda bi, i, *_: (bi, 0, i))
    return pl.pallas_call(
        kern,
        out_shape=jax.ShapeDtypeStruct((b, n, dm), F32),
        grid_spec=pltpu.PrefetchScalarGridSpec(
            num_scalar_prefetch=2,
            grid=(b, nb),
            in_specs=[pl.BlockSpec(memory_space=pl.ANY),
                      per_expert, per_expert,
                      pl.BlockSpec((None, TOKEN_BLOCK, dm), lambda bi, i, *_: (bi, i, 0)),
                      pl.BlockSpec((None, 1, dm), lambda bi, i, *_: (bi, 0, 0)),
                      pl.BlockSpec((1, dm), lambda bi, i, *_: (0, 0))],
            out_specs=pl.BlockSpec((None, TOKEN_BLOCK, dm), lambda bi, i, *_: (bi, i, 0)),
            scratch_shapes=[pltpu.VMEM((N_EXPERTS * w, SUBLANES, LANES), F32),
                            pltpu.VMEM((TOKEN_BLOCK, dm), F32),
                            pltpu.SemaphoreType.DMA((1,))]),
        compiler_params=_cparams(("arbitrary", "arbitrary"), VMEM_LIMIT),
    )(s0, npass, y, slot_en, aff_t, xm, gt2, fn)


def _rope_tables(n, positioned):
    j = np.arange(LANES) % MLA_ROPE
    half = MLA_ROPE // 2
    nf = half // 2
    inv = ROPE_BASE ** (-(np.arange(nf, dtype=np.float64)) / nf)
    if not positioned:
        return jnp.ones((n, LANES), F32), jnp.zeros((n, LANES), F32)
    t = np.arange(n)
    pos = np.where((j < half)[None, :], (t // GRID_W)[:, None], (t % GRID_W)[:, None]).astype(np.float64)
    ang = pos * inv[(j % half) % nf][None, :]
    sign = np.where((j % half) < nf, -1.0, 1.0)[None, :]
    return jnp.asarray(np.cos(ang), F32), jnp.asarray(np.sin(ang) * sign, F32)


def _swap_halves(w):
    shp = w.shape
    w4 = w.reshape(shp[:-1] + (shp[-1] // 16, 2, 8))
    return w4[..., ::-1, :].reshape(shp)


def _layer_weights(p, i):
    w_in = p['w_in'][i]
    kr = w_in[:, _C_KVC + MLA_KV_RANK:]
    w_all = jnp.concatenate([w_in[:, :_C_KVC + MLA_KV_RANK], jnp.tile(kr, (1, 4)),
                             jnp.tile(_swap_halves(kr), (1, 4))], axis=1).astype(BF16)
    w_uq = p['mla_w_uq'][i].reshape(MLA_Q_RANK, MLA_HEADS, MLA_NOPE + MLA_ROPE)
    w_uk = p['mla_w_uk'][i].reshape(MLA_KV_RANK, MLA_HEADS, MLA_NOPE)
    w_abs = jnp.einsum('qhd,khd->qhk', w_uq[..., :MLA_NOPE], w_uk, precision=lax.Precision.HIGHEST)
    w_rope = w_uq[..., MLA_NOPE:].reshape(MLA_Q_RANK, MLA_HEADS * MLA_ROPE)
    wq_all = (jnp.concatenate([w_abs.reshape(MLA_Q_RANK, MLA_HEADS * MLA_KV_RANK), w_rope, _swap_halves(w_rope)],
                              axis=1) * (MLA_NOPE + MLA_ROPE) ** -0.5).astype(BF16)
    w_uv = p['mla_w_uv'][i].reshape(MLA_KV_RANK, MLA_HEADS, MLA_V)
    wuv_pad = jnp.stack([jnp.pad(w_uv[:, hd], ((0, 0), (hd * MLA_V, GROUP_W - (hd + 1) * MLA_V)))
                         for hd in range(MLA_HEADS)]).astype(BF16)
    pool_bd = jax.scipy.linalg.block_diag(*[p['pool_w'][i][g] for g in range(len(POOL_WINDOWS))]).astype(BF16)
    rw_pad = jnp.pad(p['router_w'][i], ((0, 0), (0, LANES - N_EXPERTS)))
    return dict(w_all=w_all, wq_all=wq_all, wuv_pad=wuv_pad, pool_bd=pool_bd, rw_pad=rw_pad)


def _routing_tables(cumx, cap, w):
    b, e, nc, _ = cumx.shape
    per = TOKEN_BLOCK // LANES
    s0 = cumx[:, :, ::per, 0]
    nxt = jnp.concatenate([s0[:, :, 1:], jnp.full((b, e, 1), cap, I32)], axis=2)
    npass = (jnp.max(nxt - s0, axis=1) + (w - 1)) // w
    return s0.reshape(-1), npass.reshape(-1)


def _mixer_inputs(x, mod, lw, p, i, cos4, sin4, tm):
    sh1, sc1 = mod[0], mod[1]
    return _in_proj(x, sh1, sc1, p['norm1_g'][i][None], lw['w_all'], p['mla_q_norm'][i][None],
                    p['mla_kv_norm'][i][None], lw['wq_all'], cos4, sin4, tm)


def _moe_block(xl, groups, mod, lw, p, i, cap, w, tm, tmf, final):
    a, bb, c, d = groups
    gn = p['grp_norm'][i][:, None, :]
    xm, h2, aff_t = _merge_router(xl, a, bb, c, d, gn, p['w_out'][i], mod[2], p['norm2_g'][i][None],
                                          mod[3], mod[4], lw['rw_pad'], tm)
    slot4, cumx = _select(aff_t, cap)
    s0, npass = _routing_tables(cumx, cap, w)
    slot_en = slot4.reshape(aff_t.shape)
    xs = _dispatch(h2, slot_en, s0, npass, cap, w)
    y = _expert_ffn(xs, p['exp_w_gate'][i], p['exp_w_up'][i], p['exp_w_down'][i], cap, tmf)
    return _combine(y, slot_en, aff_t, xm, mod[5], p['final_norm'][None], s0, npass, cap, w, final)


def kernel(x, c, ctx, c_ctx, ada_w, ada_b, norm1_g, norm2_g, w_in, na_bias, pool_w, pool_scale, fno_w, mla_q_norm, mla_w_uq, mla_kv_norm, mla_w_uk, mla_w_uv, grp_norm, w_out, router_w, exp_w_gate, exp_w_up, exp_w_down, final_norm):
    p = dict(norm1_g=norm1_g, norm2_g=norm2_g, w_in=w_in, pool_w=pool_w, mla_q_norm=mla_q_norm,
             mla_w_uq=mla_w_uq, mla_kv_norm=mla_kv_norm, mla_w_uk=mla_w_uk, mla_w_uv=mla_w_uv,
             grp_norm=grp_norm, w_out=w_out, router_w=router_w, exp_w_gate=exp_w_gate, exp_w_up=exp_w_up,
             exp_w_down=exp_w_down, final_norm=final_norm)
    b, n, dm = x.shape
    lc = ctx.shape[1]
    depth = ada_w.shape[0]
    assert b + 1 <= 8
    cs = jnp.zeros((8, dm), F32).at[:b].set(c).at[b].set(c_ctx)
    mods = _adaln(cs, ada_w, ada_b)
    cos_l, sin_l = _rope_tables(n, True)
    cos_c, sin_c = _rope_tables(lc, False)
    cap_l = max(1, EC_CAPACITY_FACTOR * n // N_EXPERTS)
    cap_c = max(1, EC_CAPACITY_FACTOR * lc // N_EXPERTS)
    xl, xc = x, ctx
    for i in range(depth):
        last = i == depth - 1
        lw = _layer_weights(p, i)
        m6 = mods[i].reshape(8, 6, dm)
        mod_l = [m6[:b, j][:, None, :] for j in range(6)]
        mod_c = [jnp.broadcast_to(m6[b, j][None, None, :], (b, 1, dm)) for j in range(6)]
        qa_l, ka_l, va_l, pl_l, fl_l, q_l, kt_l, v_l = _mixer_inputs(xl, mod_l, lw, p, i, cos_l, sin_l, 512)
        qa_c, ka_c, va_c, pl_c, fl_c, q_c, kt_c, v_c = _mixer_inputs(xc, mod_c, lw, p, i, cos_c, sin_c, lc)
        bm = _na_bias_mask(na_bias[i], n // GRID_W)
        a_l = _na_attn(qa_l, ka_l, va_l, ka_c, va_c, bm)
        b_l = _pool_mix(pl_l, lw['pool_bd'], pool_scale[i][None], 1024)
        c_l = _fourier_mix(fl_l, fno_w[i])
        d_l = _latent_attn(q_l, jnp.concatenate([kt_l, kt_c], axis=2), jnp.concatenate([v_l, v_c], axis=1),
                           lw['wuv_pad'], 128, 768)
        xl = _moe_block(xl, (a_l, b_l, c_l, d_l), mod_l, lw, p, i, cap_l, 64, 512, 512, last)
        if not last:
            a_c = _dense_attn(qa_c, ka_c, va_c)
            b_c = _pool_mix(pl_c, lw['pool_bd'], pool_scale[i][None], lc)
            c_c = _fourier_mix_small(fl_c, fno_w[i])
            d_c = _latent_attn(q_c, kt_c, v_c, lw['wuv_pad'], lc, lc)
            xc = _moe_block(xc, (a_c, b_c, c_c, d_c), mod_c, lw, p, i, cap_c, cap_c, lc, cap_c, False)
    return xl
```
